```python
import jax, jax.numpy as jnp
from jax import lax
import numpy as np

D_MODEL = 1024
BATCH = 16
SEQ = 2048
DEPTH = 1

CTX_LEN = 256
GRID_W = 64
CHUNK = 128
ROWS_PER_CHUNK = CHUNK // GRID_W
GM_HEADS = 8
GM_HEAD_DIM = 128
GM_WIDTH = GM_HEADS * GM_HEAD_DIM
SSD_HEADS = 16
SSD_HEAD_DIM = 64
SSD_WIDTH = SSD_HEADS * SSD_HEAD_DIM
SSD_STATE = 128
SSD_GROUPS = 2
SSD_HPG = SSD_HEADS // SSD_GROUPS
SSD_CONV = 5
SSD_CHUNK = 128
CONV_CH = SSD_WIDTH + 2 * SSD_GROUPS * SSD_STATE
SSD_IN_WIDTH = SSD_WIDTH + CONV_CH + 2 * SSD_HEADS
MIX_WIDTH = GM_WIDTH + SSD_WIDTH
IN_WIDTH = 2 * GM_WIDTH + SSD_IN_WIDTH
FFN_HIDDEN = -(-8 * D_MODEL // (3 * 256)) * 256
N_MOD = 6
EPS = 1e-6

kernel_name = "hybrid_gmlp_ssd_dit_block"


def _rms(x):
    x32 = x.astype(jnp.float32)
    y = x32 * lax.rsqrt(jnp.mean(x32 * x32, axis=-1, keepdims=True) + EPS)
    return y.astype(x.dtype)


def _conv_centred(x, w, b):
    ch = x.shape[-1]
    y = lax.conv_general_dilated(
        x, w[:, None, :], window_strides=(1,),
        padding=[(SSD_CONV // 2, SSD_CONV // 2)],
        dimension_numbers=("NWC", "WIO", "NWC"), feature_group_count=ch)
    return y + b


def _ssd_chunked(x, dt, A, bm, cm, h0):
    b, l, _, _ = x.shape
    nc = l // SSD_CHUNK
    f32 = jnp.float32
    x = x.astype(f32).reshape(b, nc, SSD_CHUNK, SSD_GROUPS, SSD_HPG, SSD_HEAD_DIM)
    dt = dt.astype(f32).reshape(b, nc, SSD_CHUNK, SSD_GROUPS, SSD_HPG)
    bm = bm.astype(f32).reshape(b, nc, SSD_CHUNK, SSD_GROUPS, SSD_STATE)
    cm = cm.astype(f32).reshape(b, nc, SSD_CHUNK, SSD_GROUPS, SSD_STATE)
    a = dt * A.astype(f32).reshape(SSD_GROUPS, SSD_HPG)
    a_cs = jnp.cumsum(a, axis=2)
    xdt = x * dt[..., None]
    seg = a_cs[:, :, :, None] - a_cs[:, :, None, :]
    tri = jnp.tril(jnp.ones((SSD_CHUNK, SSD_CHUNK), dtype=bool))[:, :, None, None]
    decay = jnp.exp(jnp.where(tri, seg, -jnp.inf))
    cb = jnp.einsum('bctgn,bcsgn->bctsg', cm, bm)
    y_diag = jnp.einsum('bctsg,bctsge,bcsgep->bctgep', cb, decay, xdt)
    decay_to_end = jnp.exp(a_cs[:, :, -1:] - a_cs)
    states = jnp.einsum('bcsgn,bcsge,bcsgep->bcgepn', bm, decay_to_end, xdt)
    chunk_decay = jnp.exp(a_cs[:, :, -1])

    def step(h, inp):
        s, d = inp
        return d[..., None, None] * h + s, h

    h_final, h_prev = lax.scan(step, h0.astype(f32),
                               (jnp.moveaxis(states, 1, 0), jnp.moveaxis(chunk_decay, 1, 0)))
    h_prev = jnp.moveaxis(h_prev, 0, 1)
    y_off = jnp.einsum('bctgn,bcgepn,bctge->bctgep', cm, h_prev, jnp.exp(a_cs))
    y = (y_diag + y_off).reshape(b, l, SSD_HEADS, SSD_HEAD_DIM)
    return y, h_final


def _ssd_bidirectional(xs, bm, cm, dt_raw, a_log, dt_bias, st_f, st_b):
    f32 = jnp.float32
    dt_f = jax.nn.softplus((dt_raw[..., :SSD_HEADS] + dt_bias[0]).astype(f32))
    y_f, fin_f = _ssd_chunked(xs, dt_f, -jnp.exp(a_log[0].astype(f32)), bm, cm, st_f)
    dt_b = jax.nn.softplus((dt_raw[..., SSD_HEADS:] + dt_bias[1]).astype(f32))
    y_b, fin_b = _ssd_chunked(jnp.flip(xs, 1), jnp.flip(dt_b, 1),
                              -jnp.exp(a_log[1].astype(f32)),
                              jnp.flip(bm, 1), jnp.flip(cm, 1), st_b)
    return y_f + jnp.flip(y_b, 1), fin_f, fin_b


def _token_mixer(h, st_f, st_b, n_chunks, with_output, w_in, gm_norm_w, gm_ws, gm_bs,
                 conv_w, conv_b, a_log, dt_bias, d_skip, ssd_norm_w, w_out):
    b, l, _ = h.shape
    proj = h @ (w_in if with_output else w_in[:, 2 * GM_WIDTH:])
    ssd_in = proj[..., proj.shape[-1] - SSD_IN_WIDTH:]
    z, xbc, dt_raw = jnp.split(ssd_in, [SSD_WIDTH, SSD_WIDTH + CONV_CH], axis=-1)
    xbc = jax.nn.silu(_conv_centred(xbc, conv_w, conv_b))
    xs, bm, cm = jnp.split(xbc, [SSD_WIDTH, SSD_WIDTH + SSD_GROUPS * SSD_STATE], axis=-1)
    xs = xs.reshape(b, l, SSD_HEADS, SSD_HEAD_DIM)
    bm = bm.reshape(b, l, SSD_GROUPS, SSD_STATE)
    cm = cm.reshape(b, l, SSD_GROUPS, SSD_STATE)
    y, fin_f, fin_b = _ssd_bidirectional(xs, bm, cm, dt_raw, a_log, dt_bias, st_f, st_b)
    if not with_output:
        return None, fin_f, fin_b
    y = y.astype(h.dtype) + xs * d_skip[:, None]
    g = (y.reshape(b, l, SSD_WIDTH) * jax.nn.silu(z)).reshape(b, l, SSD_GROUPS, SSD_WIDTH // SSD_GROUPS)
    ssd_out = _rms(g).reshape(b, l, SSD_WIDTH) * ssd_norm_w
    u, v = jnp.split(jax.nn.gelu(proj[..., :2 * GM_WIDTH]), 2, axis=-1)
    v = _rms(v.reshape(b, l, GM_HEADS, GM_HEAD_DIM)) * gm_norm_w.reshape(GM_HEADS, GM_HEAD_DIM)
    v = v.reshape(b, n_chunks, CHUNK, GM_HEADS, GM_HEAD_DIM)
    mixed = jnp.einsum('hts,bcshd->bcthd', gm_ws, v) + gm_bs.T[:, :, None]
    gm_out = u * mixed.reshape(b, l, GM_WIDTH)
    out = jnp.concatenate([gm_out, ssd_out], axis=-1) @ w_out
    return out, fin_f, fin_b


def _swiglu(h, w13, w2):
    gate, up = jnp.split(h @ w13, 2, axis=-1)
    return (jax.nn.silu(gate) * up) @ w2


def setup_inputs(seed: int = 0) -> dict:
    key = jax.random.key(seed)
    ks = jax.random.split(key, 24)
    f32 = jnp.float32

    def nrm(k, shape, scale):
        return jax.random.normal(k, shape, f32) * scale

    dt0 = jnp.exp(jax.random.uniform(ks[13], (DEPTH, 2, SSD_HEADS), f32,
                                     np.log(1e-3), np.log(1e-1)))
    return {
        "x": nrm(ks[0], (BATCH, SEQ, D_MODEL), 1.0),
        "c": nrm(ks[1], (BATCH, D_MODEL), 1.0),
        "ctx": nrm(ks[2], (BATCH, CTX_LEN, D_MODEL), 1.0),
        "c_ctx": nrm(ks[3], (D_MODEL,), 1.0),
        "w_ada": nrm(ks[4], (DEPTH, D_MODEL, N_MOD * D_MODEL), 0.5 * D_MODEL ** -0.5),
        "b_ada": nrm(ks[5], (DEPTH, N_MOD * D_MODEL), 0.02),
        "norm1_w": 1.0 + nrm(ks[6], (DEPTH, D_MODEL), 0.02),
        "w_in": nrm(ks[7], (DEPTH, D_MODEL, IN_WIDTH), D_MODEL ** -0.5),
        "gm_norm_w": 1.0 + nrm(ks[8], (DEPTH, GM_WIDTH), 0.02),
        "gm_ws": nrm(ks[9], (DEPTH, GM_HEADS, CHUNK, CHUNK), CHUNK ** -0.5),
        "gm_bs": 1.0 + nrm(ks[10], (DEPTH, GM_HEADS, CHUNK), 0.02),
        "conv_w": nrm(ks[11], (DEPTH, SSD_CONV, CONV_CH), SSD_CONV ** -0.5),
        "conv_b": nrm(ks[12], (DEPTH, CONV_CH), 0.02),
        "ssd_A_log": jnp.log(jax.random.uniform(ks[14], (DEPTH, 2, SSD_HEADS), f32, 1.0, 16.0)),
        "ssd_dt_bias": dt0 + jnp.log(-jnp.expm1(-dt0)),
        "ssd_D": 1.0 + nrm(ks[15], (DEPTH, SSD_HEADS), 0.02),
        "ssd_norm_w": 1.0 + nrm(ks[16], (DEPTH, SSD_WIDTH), 0.02),
        "w_out": nrm(ks[17], (DEPTH, MIX_WIDTH, D_MODEL), MIX_WIDTH ** -0.5),
        "norm2_w": 1.0 + nrm(ks[18], (DEPTH, D_MODEL), 0.02),
        "ffn_w13": nrm(ks[19], (DEPTH, D_MODEL, 2 * FFN_HIDDEN), D_MODEL ** -0.5),
        "ffn_w2": nrm(ks[20], (DEPTH, FFN_HIDDEN, D_MODEL), FFN_HIDDEN ** -0.5),
        "final_norm_w": 1.0 + nrm(ks[21], (D_MODEL,), 0.02),
    }


def reference(x, c, ctx, c_ctx, w_ada, b_ada, norm1_w, w_in, gm_norm_w, gm_ws, gm_bs,
              conv_w, conv_b, ssd_A_log, ssd_dt_bias, ssd_D, ssd_norm_w, w_out,
              norm2_w, ffn_w13, ffn_w2, final_norm_w):
    rows = x.shape[1] // GRID_W
    lat_chunks = rows // ROWS_PER_CHUNK
    ctx_chunks = ctx.shape[1] // CHUNK
    n_b = ctx.shape[0]
    zero_state = jnp.zeros((n_b, SSD_GROUPS, SSD_HPG, SSD_HEAD_DIM, SSD_STATE), jnp.float32)
    for i in range(DEPTH):
        last = i == DEPTH - 1
        m_lat = (jax.nn.silu(c) @ w_ada[i] + b_ada[i])[:, None, :]
        m_ctx = jax.nn.silu(c_ctx) @ w_ada[i] + b_ada[i]
        sh1, sc1, g1, sh2, sc2, g2 = jnp.split(m_lat, N_MOD, axis=-1)
        csh1, csc1, cg1, csh2, csc2, cg2 = jnp.split(m_ctx, N_MOD, axis=-1)
        mix_params = (w_in[i], gm_norm_w[i], gm_ws[i], gm_bs[i], conv_w[i], conv_b[i],
                      ssd_A_log[i], ssd_dt_bias[i], ssd_D[i], ssd_norm_w[i], w_out[i])
        hc = _rms(ctx) * norm1_w[i] * (1.0 + csc1) + csh1
        ctx_mix, st_f, st_b = _token_mixer(hc, zero_state, zero_state, ctx_chunks,
                                           not last, *mix_params)
        hl = _rms(x) * norm1_w[i] * (1.0 + sc1) + sh1
        lat_mix, _, _ = _token_mixer(hl, st_f, st_b, lat_chunks, True, *mix_params)
        x = x + g1 * lat_mix
        hl = _rms(x) * norm2_w[i] * (1.0 + sc2) + sh2
        x = x + g2 * _swiglu(hl, ffn_w13[i], ffn_w2[i])
        if not last:
            ctx = ctx + cg1 * ctx_mix
            hc = _rms(ctx) * norm2_w[i] * (1.0 + csc2) + csh2
            ctx = ctx + cg2 * _swiglu(hc, ffn_w13[i], ffn_w2[i])
    return _rms(x) * final_norm_w
```

```python
import functools

import jax
import jax.numpy as jnp
from jax import lax
from jax.experimental import pallas as pl
from jax.experimental.pallas import tpu as pltpu

F32 = jnp.float32
BF16 = jnp.bfloat16

D_MODEL = 1024
N_MOD = 6
EPS = 1e-6
CHUNK = 128
GM_HEADS = 8
GM_HEAD_DIM = 128
GM_WIDTH = GM_HEADS * GM_HEAD_DIM
SSD_HEADS = 16
SSD_HEAD_DIM = 64
SSD_WIDTH = SSD_HEADS * SSD_HEAD_DIM
SSD_STATE = 128
SSD_GROUPS = 2
SSD_HPG = SSD_HEADS // SSD_GROUPS
SSD_CONV = 5
BC_WIDTH = SSD_GROUPS * SSD_STATE
CONV_CH = SSD_WIDTH + 2 * BC_WIDTH
FFN_HIDDEN = 2816

LANES = 128
BF16_ROWS = 16
VMEM_LIMIT = 56 * 1024 * 1024

HALO = BF16_ROWS
PAIR = LANES // SSD_HEAD_DIM
N_PAIRS = SSD_HEADS // PAIR
PAIRS_PER_GROUP = SSD_HPG // PAIR


def _silu(x):
    return x / (1.0 + jnp.exp(-x))


def _gelu_tanh(x):
    c = 0.7978845608028654
    return x * (0.5 * (1.0 + jnp.tanh(c * (x + 0.044715 * (x * x * x)))))


def _softplus(x):
    return jnp.maximum(x, 0.0) + jnp.log1p(jnp.exp(-jnp.abs(x)))


def _rms_mod(x, scale, shift):
    ms = jnp.mean(x * x, axis=-1, keepdims=True)
    return x * lax.rsqrt(ms + EPS) * scale + shift


def _dot(a, b):
    return jnp.dot(a, b, preferred_element_type=F32)


def _dot_nt(a, b):
    return lax.dot_general(a, b, (((1,), (1,)), ((), ())), preferred_element_type=F32)


def _split3(x):
    hi = x.astype(BF16)
    r1 = x - hi.astype(F32)
    mid = r1.astype(BF16)
    lo = (r1 - mid.astype(F32)).astype(BF16)
    return hi, mid, lo


def _const_spec(shape):
    nd = len(shape)
    return pl.BlockSpec(shape, lambda *_: (0,) * nd, pipeline_mode=pl.Buffered(1))


def _ada_kernel(c_ref, w_ref, b_ref, o_ref):
    s = _silu(c_ref[...]).astype(BF16)
    o_ref[...] = _dot(s, w_ref[...].astype(BF16)) + b_ref[...]


def _ada_call(c_pad, w_ada, b_ada):
    rows = c_pad.shape[0]
    n_out = w_ada.shape[-1]
    tn = D_MODEL
    return pl.pallas_call(
        _ada_kernel,
        grid=(n_out // tn,),
        in_specs=[
            pl.BlockSpec((rows, D_MODEL), lambda j: (0, 0)),
            pl.BlockSpec((None, D_MODEL, tn), lambda j: (0, 0, j)),
            pl.BlockSpec((None, 1, tn), lambda j: (0, 0, j)),
        ],
        out_specs=pl.BlockSpec((rows, tn), lambda j: (0, j)),
        out_shape=jax.ShapeDtypeStruct((rows, n_out), F32),
        compiler_params=pltpu.CompilerParams(
            dimension_semantics=("arbitrary",), vmem_limit_bytes=VMEM_LIMIT),
        name="ada_mod",
    )(c_pad, w_ada, b_ada.reshape(b_ada.shape[0], 1, n_out))


def _inproj_kernel(with_gm, tm, nt, *refs):
    if with_gm:
        (xm_ref, xp_ref, xn_ref, nw_ref, sc_ref, sh_ref, wxbc_ref, wdtT_ref, cw_ref, cb_ref,
         wgm_ref, wz_ref, gnw_ref, ws_ref, bias_ref,
         xbc_ref, dtT_ref, gmo_ref, z_ref) = refs
    else:
        (xm_ref, xp_ref, xn_ref, nw_ref, sc_ref, sh_ref, wxbc_ref, wdtT_ref, cw_ref, cb_ref,
         xbc_ref, dtT_ref) = refs
    t = pl.program_id(1)
    scale = nw_ref[...] * (1.0 + sc_ref[0])
    shift = sh_ref[0]
    hm = _rms_mod(xm_ref[0], scale, shift).astype(BF16)
    hp = _rms_mod(xp_ref[0], scale, shift).astype(BF16)
    hn = _rms_mod(xn_ref[0], scale, shift).astype(BF16)
    h_ext = jnp.concatenate([hp, hm, hn], axis=0)
    rows = tm + 2 * HALO

    pre = _dot(h_ext, wxbc_ref[...])
    ridx = lax.broadcasted_iota(jnp.int32, (rows, 1), 0)
    inside = jnp.logical_and(jnp.logical_or(ridx >= HALO, t > 0),
                             jnp.logical_or(ridx < tm + HALO, t < nt - 1))
    pre = jnp.where(inside, pre, 0.0)
    half = SSD_CONV // 2
    acc = cb_ref[...] + cw_ref[half:half + 1, :] * pre[HALO:HALO + tm]
    for k in range(SSD_CONV):
        if k == half:
            continue
        shifted = pltpu.roll(pre, (half - k) % rows, 0)
        acc = acc + cw_ref[k:k + 1, :] * shifted[HALO:HALO + tm]
    xbc_ref[0] = _silu(acc).astype(BF16)

    dtT = _dot_nt(wdtT_ref[...], hm)
    for j in range(tm // CHUNK):
        dtT_ref[0, j] = dtT[:, j * CHUNK:(j + 1) * CHUNK]

    if with_gm:
        z_ref[0] = _dot(hm, wz_ref[...]).astype(BF16)
        gm = _gelu_tanh(_dot(hm, wgm_ref[...]))
        nj = tm // CHUNK
        for h in range(GM_HEADS):
            lo = h * GM_HEAD_DIM
            v = gm[:, GM_WIDTH + lo:GM_WIDTH + lo + GM_HEAD_DIM]
            ms = jnp.mean(v * v, axis=-1, keepdims=True)
            vn = (v * lax.rsqrt(ms + EPS) * gnw_ref[:, lo:lo + GM_HEAD_DIM]).astype(BF16)
            rhs = jnp.concatenate([vn[j * CHUNK:(j + 1) * CHUNK] for j in range(nj)], axis=1)
            mixed = _dot(ws_ref[h], rhs)
            for j in range(nj):
                u = gm[j * CHUNK:(j + 1) * CHUNK, lo:lo + GM_HEAD_DIM]
                o = u * (mixed[:, j * GM_HEAD_DIM:(j + 1) * GM_HEAD_DIM]
                         + bias_ref[:, lo:lo + GM_HEAD_DIM])
                gmo_ref[0, j * CHUNK:(j + 1) * CHUNK, lo:lo + GM_HEAD_DIM] = o.astype(BF16)


def _inproj_call(x, norm_w, sc, sh, w_xbc, w_dtT, conv_w, conv_b, gm_args, tm):
    b, l, d = x.shape
    nt = l // tm
    nc_tile = tm // CHUNK
    hb = tm // HALO
    n_hb = l // HALO
    per_batch = sc.shape[0] == b
    mod_map = (lambda i, t: (i, 0, 0)) if per_batch else (lambda i, t: (0, 0, 0))
    with_gm = gm_args is not None

    in_specs = [
        pl.BlockSpec((1, tm, d), lambda i, t: (i, t, 0)),
        pl.BlockSpec((1, HALO, d), lambda i, t: (i, jnp.maximum(t * hb - 1, 0), 0)),
        pl.BlockSpec((1, HALO, d), lambda i, t: (i, jnp.minimum((t + 1) * hb, n_hb - 1), 0)),
        _const_spec((1, d)),
        pl.BlockSpec((1, 1, d), mod_map),
        pl.BlockSpec((1, 1, d), mod_map),
        _const_spec(w_xbc.shape),
        _const_spec(w_dtT.shape),
        _const_spec(conv_w.shape),
        _const_spec(conv_b.shape),
    ]
    args = [x, x, x, norm_w, sc, sh, w_xbc, w_dtT, conv_w, conv_b]
    out_specs = [
        pl.BlockSpec((1, tm, CONV_CH), lambda i, t: (i, t, 0)),
        pl.BlockSpec((1, nc_tile, 2 * SSD_HEADS, CHUNK), lambda i, t: (i, t, 0, 0)),
    ]
    out_shape = [
        jax.ShapeDtypeStruct((b, l, CONV_CH), BF16),
        jax.ShapeDtypeStruct((b, l // CHUNK, 2 * SSD_HEADS, CHUNK), F32),
    ]
    if with_gm:
        w_gm, w_z, gm_norm_w, gm_ws, gm_bias = gm_args
        in_specs += [_const_spec(w_gm.shape), _const_spec(w_z.shape), _const_spec(gm_norm_w.shape),
                     _const_spec(gm_ws.shape), _const_spec(gm_bias.shape)]
        args += [w_gm, w_z, gm_norm_w, gm_ws, gm_bias]
        out_specs += [pl.BlockSpec((1, tm, GM_WIDTH), lambda i, t: (i, t, 0)),
                      pl.BlockSpec((1, tm, SSD_WIDTH), lambda i, t: (i, t, 0))]
        out_shape += [jax.ShapeDtypeStruct((b, l, GM_WIDTH), BF16),
                      jax.ShapeDtypeStruct((b, l, SSD_WIDTH), BF16)]
    return pl.pallas_call(
        functools.partial(_inproj_kernel, with_gm, tm, nt),
        grid=(b, nt),
        in_specs=in_specs,
        out_specs=out_specs,
        out_shape=out_shape,
        compiler_params=pltpu.CompilerParams(
            dimension_semantics=("arbitrary", "arbitrary"), vmem_limit_bytes=VMEM_LIMIT),
        name="inproj_lat" if with_gm else "inproj_ctx",
    )(*args)


def _dt_terms(dtT_raw, dtb_col, alog_col, tri_le, tri_ge):
    dt = _softplus(dtT_raw + dtb_col)
    a = dt * (-jnp.exp(alog_col))
    hi, mid, lo = _split3(a)
    cs = _dot(hi, tri_le) + _dot(mid, tri_le) + _dot(lo, tri_le)
    rs = _dot(hi, tri_ge) + _dot(mid, tri_ge) + _dot(lo, tri_ge)
    tot = jnp.sum(a, axis=-1, keepdims=True)
    return dt, a, cs, rs, tot


def _blockdiag(x_pair, lane_lo):
    zero = jnp.zeros_like(x_pair)
    return jnp.concatenate([jnp.where(lane_lo, x_pair, zero), jnp.where(lane_lo, zero, x_pair)], axis=0)


def _ssd_kernel(nc, ncc, xbc_ref, z_ref, dtT_ref, xbcc_ref, dtTc_ref, dtb_ref, alog_ref, dexp_ref, nw_ref,
                out_ref, hf_all, hb_all, hf, hb):
    C = CHUNK
    H = SSD_HEADS
    r_i = lax.broadcasted_iota(jnp.int32, (C, C), 0)
    c_i = lax.broadcasted_iota(jnp.int32, (C, C), 1)
    tri_le = (r_i <= c_i).astype(BF16)
    tri_ge = (r_i >= c_i).astype(BF16)
    lower = c_i <= r_i
    diag = c_i == r_i
    lane_lo = lax.broadcasted_iota(jnp.int32, (C, LANES), 1) < SSD_HEAD_DIM
    e_i = lax.broadcasted_iota(jnp.int32, (2 * H, SSD_WIDTH), 0)
    l_i = lax.broadcasted_iota(jnp.int32, (2 * H, SSD_WIDTH), 1) // SSD_HEAD_DIM
    sel_f = e_i == l_i
    sel_b = e_i == l_i + H
    dtb_col = dtb_ref[...]
    alog_col = alog_ref[...]

    def state_update(x_ref, dtT_raw, r0, h_ref, fwd):
        dt, a, cs, rs, tot = _dt_terms(dtT_raw, dtb_col, alog_col, tri_le, tri_ge)
        w = dt * jnp.exp(tot - (cs if fwd else rs))
        dec = jnp.exp(jnp.sum(jnp.where(sel_f if fwd else sel_b, tot, 0.0), axis=0, keepdims=True))
        off = 0 if fwd else H
        for g in range(SSD_GROUPS):
            b_g = x_ref[0, pl.ds(r0, C), SSD_WIDTH + g * SSD_STATE:SSD_WIDTH + (g + 1) * SSD_STATE]
            bT = b_g.astype(F32).T
            for pp in range(PAIRS_PER_GROUP):
                p = g * PAIRS_PER_GROUP + pp
                e0 = p * PAIR
                lhs = jnp.concatenate(
                    [bT * w[off + e0 + q:off + e0 + q + 1, :] for q in range(PAIR)], axis=1).astype(BF16)
                x_pair = x_ref[0, pl.ds(r0, C), p * LANES:(p + 1) * LANES]
                s_pair = _dot(lhs, _blockdiag(x_pair, lane_lo))
                cols = slice(p * LANES, (p + 1) * LANES)
                h_ref[:, cols] = dec[:, cols] * h_ref[:, cols] + s_pair

    hf[...] = jnp.zeros_like(hf)
    hb[...] = jnp.zeros_like(hb)
    for c in range(ncc):
        state_update(xbcc_ref, dtTc_ref[0, c], c * C, hf, True)
    for c in range(ncc - 1, -1, -1):
        state_update(xbcc_ref, dtTc_ref[0, c], c * C, hb, False)

    def states_body(i, carry):
        cf = i
        hf_all[cf] = hf[...].astype(BF16)
        state_update(xbc_ref, dtT_ref[0, cf], pl.multiple_of(cf * C, C), hf, True)
        cb = nc - 1 - i
        hb_all[cb] = hb[...].astype(BF16)
        state_update(xbc_ref, dtT_ref[0, cb], pl.multiple_of(cb * C, C), hb, False)
        return carry

    lax.fori_loop(0, nc, states_body, 0)

    def out_body(c, carry):
        r0 = pl.multiple_of(c * C, C)
        dt, a, cs, rs, tot = _dt_terms(dtT_ref[0, c], dtb_col, alog_col, tri_le, tri_ge)
        hf_c = hf_all[c]
        hb_c = hb_all[c]
        for g in range(SSD_GROUPS):
            b_g = xbc_ref[0, pl.ds(r0, C), SSD_WIDTH + g * SSD_STATE:SSD_WIDTH + (g + 1) * SSD_STATE]
            c_g = xbc_ref[0, pl.ds(r0, C),
                          SSD_WIDTH + BC_WIDTH + g * SSD_STATE:SSD_WIDTH + BC_WIDTH + (g + 1) * SSD_STATE]
            cb = _dot_nt(c_g, b_g)
            c_f32 = c_g.astype(F32)
            gated = []
            ssq = jnp.zeros((C, 1), F32)
            for pp in range(PAIRS_PER_GROUP):
                p = g * PAIRS_PER_GROUP + pp
                m_blocks, cf_blocks, cb_blocks = [], [], []
                for q in range(PAIR):
                    e = p * PAIR + q
                    row_f = jnp.broadcast_to(cs[e:e + 1, :], (C, C))
                    row_b = jnp.broadcast_to(rs[H + e:H + e + 1, :], (C, C))
                    col_f = row_f.T
                    col_b = row_b.T
                    arg = jnp.where(lower, col_f - row_f, col_b - row_b)
                    dts = jnp.where(lower, dt[e:e + 1, :], dt[H + e:H + e + 1, :])
                    decay = jnp.exp(arg) * dts + jnp.where(diag, dt[H + e:H + e + 1, :], 0.0)
                    m_blocks.append(cb * decay)
                    cf_blocks.append(c_f32 * jnp.exp(col_f))
                    cb_blocks.append(c_f32 * jnp.exp(col_b))
                lhs = jnp.concatenate(m_blocks + cf_blocks + cb_blocks, axis=1).astype(BF16)
                cols = slice(p * LANES, (p + 1) * LANES)
                x_pair = xbc_ref[0, pl.ds(r0, C), cols]
                rhs = jnp.concatenate([_blockdiag(x_pair, lane_lo), _blockdiag(hf_c[:, cols], lane_lo),
                                       _blockdiag(hb_c[:, cols], lane_lo)], axis=0)
                y = _dot(lhs, rhs) + x_pair.astype(F32) * dexp_ref[:, cols]
                gz = y * _silu(z_ref[0, pl.ds(r0, C), cols].astype(F32))
                ssq = ssq + jnp.sum(gz * gz, axis=-1, keepdims=True)
                gated.append(gz)
            inv = lax.rsqrt(ssq * (1.0 / (SSD_WIDTH // SSD_GROUPS)) + EPS)
            for pp in range(PAIRS_PER_GROUP):
                p = g * PAIRS_PER_GROUP + pp
                cols = slice(p * LANES, (p + 1) * LANES)
                out_ref[0, pl.ds(r0, C), cols] = (gated[pp] * inv * nw_ref[:, cols]).astype(BF16)
        return carry

    lax.fori_loop(0, nc, out_body, 0)


def _ssd_call(xbc, z, dtT, xbc_c, dtT_c, dtb_col, alog_col, d_exp, norm_w):
    b, l, _ = xbc.shape
    lc = xbc_c.shape[1]
    nc, ncc = l // CHUNK, lc // CHUNK
    return pl.pallas_call(
        functools.partial(_ssd_kernel, nc, ncc),
        grid=(b,),
        in_specs=[
            pl.BlockSpec((1, l, CONV_CH), lambda i: (i, 0, 0)),
            pl.BlockSpec((1, l, SSD_WIDTH), lambda i: (i, 0, 0)),
            pl.BlockSpec((1, nc, 2 * SSD_HEADS, CHUNK), lambda i: (i, 0, 0, 0)),
            pl.BlockSpec((1, lc, CONV_CH), lambda i: (i, 0, 0)),
            pl.BlockSpec((1, ncc, 2 * SSD_HEADS, CHUNK), lambda i: (i, 0, 0, 0)),
            _const_spec(dtb_col.shape),
            _const_spec(alog_col.shape),
            _const_spec(d_exp.shape),
            _const_spec(norm_w.shape),
        ],
        out_specs=pl.BlockSpec((1, l, SSD_WIDTH), lambda i: (i, 0, 0)),
        out_shape=jax.ShapeDtypeStruct((b, l, SSD_WIDTH), BF16),
        scratch_shapes=[
            pltpu.VMEM((nc, SSD_STATE, SSD_WIDTH), BF16),
            pltpu.VMEM((nc, SSD_STATE, SSD_WIDTH), BF16),
            pltpu.VMEM((SSD_STATE, SSD_WIDTH), F32),
            pltpu.VMEM((SSD_STATE, SSD_WIDTH), F32),
        ],
        compiler_params=pltpu.CompilerParams(
            dimension_semantics=("arbitrary",), vmem_limit_bytes=VMEM_LIMIT),
        name="ssd_scan",
    )(xbc, z, dtT, xbc_c, dtT_c, dtb_col, alog_col, d_exp, norm_w)


def _out_ffn_kernel(n_hid_chunks, x_ref, gm_ref, ssd_ref, g1_ref, sc2_ref, sh2_ref, g2_ref, n2w_ref, fnw_ref,
                    woa_ref, wob_ref, w1_ref, w3_ref, w2_ref, o_ref):
    mix = _dot(gm_ref[0], woa_ref[...]) + _dot(ssd_ref[0], wob_ref[...])
    x1 = x_ref[0] + g1_ref[0] * mix
    h = _rms_mod(x1, n2w_ref[...] * (1.0 + sc2_ref[0]), sh2_ref[0]).astype(BF16)
    hc = FFN_HIDDEN // n_hid_chunks
    ffn = None
    for j in range(n_hid_chunks):
        cols = slice(j * hc, (j + 1) * hc)
        gate = _dot(h, w1_ref[:, cols])
        up = _dot(h, w3_ref[:, cols])
        part = _dot((_silu(gate) * up).astype(BF16), w2_ref[cols, :])
        ffn = part if ffn is None else ffn + part
    x2 = x1 + g2_ref[0] * ffn
    ms = jnp.mean(x2 * x2, axis=-1, keepdims=True)
    o_ref[0] = x2 * lax.rsqrt(ms + EPS) * fnw_ref[...]


def _out_ffn_call(x, gm_out, ssd_out, g1, sc2, sh2, g2, norm2_w, final_w, wo_a, wo_b, w1, w3, w2, tm):
    b, l, d = x.shape
    tile = lambda width: pl.BlockSpec((1, tm, width), lambda i, t: (i, t, 0))
    mod = pl.BlockSpec((1, 1, d), lambda i, t: (i, 0, 0))
    return pl.pallas_call(
        functools.partial(_out_ffn_kernel, 2),
        grid=(b, l // tm),
        in_specs=[tile(d), tile(GM_WIDTH), tile(SSD_WIDTH), mod, mod, mod, mod,
                  _const_spec(norm2_w.shape), _const_spec(final_w.shape),
                  _const_spec(wo_a.shape), _const_spec(wo_b.shape),
                  _const_spec(w1.shape), _const_spec(w3.shape), _const_spec(w2.shape)],
        out_specs=tile(d),
        out_shape=jax.ShapeDtypeStruct((b, l, d), F32),
        compiler_params=pltpu.CompilerParams(
            dimension_semantics=("arbitrary", "arbitrary"), vmem_limit_bytes=VMEM_LIMIT),
        name="outproj_ffn",
    )(x, gm_out, ssd_out, g1, sc2, sh2, g2, norm2_w, final_w, wo_a, wo_b, w1, w3, w2)


def kernel(x, c, ctx, c_ctx, w_ada, b_ada, norm1_w, w_in, gm_norm_w, gm_ws, gm_bs, conv_w, conv_b,
           ssd_A_log, ssd_dt_bias, ssd_D, ssd_norm_w, w_out, norm2_w, ffn_w13, ffn_w2, final_norm_w):
    b, l, d = x.shape
    assert w_ada.shape[0] == 1, "single-layer block"
    assert d == D_MODEL and l % 512 == 0 and ctx.shape[1] % CHUNK == 0

    n_rows = -(-(b + 1) // 8) * 8
    c_pad = jnp.zeros((n_rows, d), F32).at[:b].set(c).at[b].set(c_ctx)
    mod = _ada_call(c_pad, w_ada, b_ada)
    sh1, sc1, g1, sh2, sc2, g2 = [m.reshape(b, 1, d) for m in jnp.split(mod[:b], N_MOD, axis=-1)]
    csh1, csc1 = [m.reshape(1, 1, d) for m in jnp.split(mod[b:b + 1], N_MOD, axis=-1)[:2]]

    w_in0 = w_in[0].astype(BF16)
    w_gm = w_in0[:, :2 * GM_WIDTH]
    w_z = w_in0[:, 2 * GM_WIDTH:2 * GM_WIDTH + SSD_WIDTH]
    w_xbc = w_in0[:, 2 * GM_WIDTH + SSD_WIDTH:2 * GM_WIDTH + SSD_WIDTH + CONV_CH]
    w_dtT = w_in0[:, 2 * GM_WIDTH + SSD_WIDTH + CONV_CH:].T
    n1w = norm1_w[0].reshape(1, d)
    cw = conv_w[0]
    cbias = conv_b[0].reshape(1, CONV_CH)
    gm_bias = jnp.repeat(gm_bs[0].T, GM_HEAD_DIM, axis=1)
    gm_args = (w_gm, w_z, gm_norm_w[0].reshape(1, GM_WIDTH), gm_ws[0].astype(BF16), gm_bias)
    dtb_col = ssd_dt_bias[0].reshape(2 * SSD_HEADS, 1)
    alog_col = ssd_A_log[0].reshape(2 * SSD_HEADS, 1)
    d_exp = jnp.repeat(ssd_D[0], SSD_HEAD_DIM).reshape(1, SSD_WIDTH)
    w_out0 = w_out[0].astype(BF16)
    w13 = ffn_w13[0].astype(BF16)

    xbc_c, dtT_c = _inproj_call(ctx, n1w, csc1, csh1, w_xbc, w_dtT, cw, cbias, None, ctx.shape[1])
    xbc, dtT, gm_out, z = _inproj_call(x, n1w, sc1, sh1, w_xbc, w_dtT, cw, cbias, gm_args, 512)
    ssd_out = _ssd_call(xbc, z, dtT, xbc_c, dtT_c, dtb_col, alog_col, d_exp,
                        ssd_norm_w[0].reshape(1, SSD_WIDTH))
    return _out_ffn_call(x, gm_out, ssd_out, g1, sc2, sh2, g2, norm2_w[0].reshape(1, d),
                         final_norm_w.reshape(1, d), w_out0[:GM_WIDTH], w_out0[GM_WIDTH:],
                         w13[:, :FFN_HIDDEN], w13[:, FFN_HIDDEN:], ffn_w2[0].astype(BF16), 512)
```

```python
import functools
import math

import jax
import jax.numpy as jnp
from jax import lax
from jax.experimental import pallas as pl
from jax.experimental.pallas import tpu as pltpu

F32 = jnp.float32
BF16 = jnp.bfloat16

D_MODEL = 1024
N_MOD = 6
EPS = 1e-6
CHUNK = 128
GM_HEADS = 8
GM_HEAD_DIM = 128
GM_WIDTH = GM_HEADS * GM_HEAD_DIM
SSD_HEADS = 16
SSD_HEAD_DIM = 64
SSD_WIDTH = SSD_HEADS * SSD_HEAD_DIM
SSD_STATE = 128
SSD_GROUPS = 2
SSD_HPG = SSD_HEADS // SSD_GROUPS
SSD_GROUP_WIDTH = SSD_WIDTH // SSD_GROUPS
SSD_CONV = 5
BC_WIDTH = SSD_GROUPS * SSD_STATE
CONV_CH = SSD_WIDTH + 2 * BC_WIDTH
FFN_HIDDEN = 2816

LANES = 128
BF16_ROWS = 16
MXU_DEPTH = 256
VMEM_LIMIT = 56 * 1024 * 1024

HALO = BF16_ROWS
PAIR = LANES // SSD_HEAD_DIM
N_PAIRS = SSD_HEADS // PAIR
PAIRS_PER_GROUP = SSD_HPG // PAIR
HALF = CHUNK // PAIR
LOG2E = 1.0 / math.log(2.0)


def _silu(x):
    return x / (1.0 + jnp.exp(-x))


def _gelu_tanh(x):
    c = 0.7978845608028654
    return x * (0.5 * (1.0 + jnp.tanh(c * (x + 0.044715 * (x * x * x)))))


def _softplus(x):
    return jnp.maximum(x, 0.0) + jnp.log1p(jnp.exp(-jnp.abs(x)))


def _rms_mod(x, scale, shift):
    ms = jnp.mean(x * x, axis=-1, keepdims=True)
    return x * lax.rsqrt(ms + EPS) * scale + shift


def _dot(a, b):
    return jnp.dot(a, b, preferred_element_type=F32)


def _dot_nt(a, b):
    return lax.dot_general(a, b, (((1,), (1,)), ((), ())), preferred_element_type=F32)


def _split3(x):
    hi = x.astype(BF16)
    r1 = x - hi.astype(F32)
    mid = r1.astype(BF16)
    lo = (r1 - mid.astype(F32)).astype(BF16)
    return hi, mid, lo


def _const_spec(shape):
    nd = len(shape)
    return pl.BlockSpec(shape, lambda *_: (0,) * nd, pipeline_mode=pl.Buffered(1))


def _ada_kernel(c_ref, w_ref, b_ref, o_ref):
    s = _silu(c_ref[...]).astype(BF16)
    o_ref[...] = _dot(s, w_ref[...].astype(BF16)) + b_ref[...]


def _ada_call(c_pad, w_ada, b_ada):
    rows = c_pad.shape[0]
    n_out = w_ada.shape[-1]
    tn = D_MODEL
    return pl.pallas_call(
        _ada_kernel,
        grid=(n_out // tn,),
        in_specs=[
            pl.BlockSpec((rows, D_MODEL), lambda j: (0, 0)),
            pl.BlockSpec((None, D_MODEL, tn), lambda j: (0, 0, j)),
            pl.BlockSpec((None, 1, tn), lambda j: (0, 0, j)),
        ],
        out_specs=pl.BlockSpec((rows, tn), lambda j: (0, j)),
        out_shape=jax.ShapeDtypeStruct((rows, n_out), F32),
        compiler_params=pltpu.CompilerParams(
            dimension_semantics=("arbitrary",), vmem_limit_bytes=VMEM_LIMIT),
        name="ada_mod",
    )(c_pad, w_ada, b_ada.reshape(b_ada.shape[0], 1, n_out))


def _inproj_kernel(with_gm, tm, nt, *refs):
    if with_gm:
        (xm_ref, xp_ref, xn_ref, nw_ref, sc_ref, sh_ref, wxbc_ref, wdtT_ref, cw_ref, cb_ref,
         wgm_ref, wz_ref, gnw_ref, ws_ref, bias_ref,
         xbc_ref, dtT_ref, gmo_ref, z_ref) = refs
    else:
        (xm_ref, xp_ref, xn_ref, nw_ref, sc_ref, sh_ref, wxbc_ref, wdtT_ref, cw_ref, cb_ref,
         xbc_ref, dtT_ref) = refs
    t = pl.program_id(1)
    scale = nw_ref[...] * (1.0 + sc_ref[0])
    shift = sh_ref[0]
    hm = _rms_mod(xm_ref[0], scale, shift).astype(BF16)
    hp = _rms_mod(xp_ref[0], scale, shift).astype(BF16)
    hn = _rms_mod(xn_ref[0], scale, shift).astype(BF16)
    h_ext = jnp.concatenate([hp, hm, hn], axis=0)
    rows = tm + 2 * HALO

    pre = _dot(h_ext, wxbc_ref[...])
    ridx = lax.broadcasted_iota(jnp.int32, (rows, 1), 0)
    inside = jnp.logical_and(jnp.logical_or(ridx >= HALO, t > 0),
                             jnp.logical_or(ridx < tm + HALO, t < nt - 1))
    pre = jnp.where(inside, pre, 0.0)
    half = SSD_CONV // 2
    acc = cb_ref[...] + cw_ref[half:half + 1, :] * pre[HALO:HALO + tm]
    for k in range(SSD_CONV):
        if k == half:
            continue
        shifted = pltpu.roll(pre, (half - k) % rows, 0)
        acc = acc + cw_ref[k:k + 1, :] * shifted[HALO:HALO + tm]
    xbc_ref[0] = _silu(acc).astype(BF16)

    dtT = _dot_nt(wdtT_ref[...], hm)
    for j in range(tm // CHUNK):
        dtT_ref[0, j] = dtT[:, j * CHUNK:(j + 1) * CHUNK]

    if with_gm:
        z_ref[0] = _dot(hm, wz_ref[...]).astype(BF16)
        gm = _gelu_tanh(_dot(hm, wgm_ref[...]))
        nj = tm // CHUNK
        for h in range(GM_HEADS):
            lo = h * GM_HEAD_DIM
            v = gm[:, GM_WIDTH + lo:GM_WIDTH + lo + GM_HEAD_DIM]
            ms = jnp.mean(v * v, axis=-1, keepdims=True)
            vn = (v * lax.rsqrt(ms + EPS) * gnw_ref[:, lo:lo + GM_HEAD_DIM]).astype(BF16)
            rhs = jnp.concatenate([vn[j * CHUNK:(j + 1) * CHUNK] for j in range(nj)], axis=1)
            mixed = _dot(ws_ref[h], rhs)
            for j in range(nj):
                u = gm[j * CHUNK:(j + 1) * CHUNK, lo:lo + GM_HEAD_DIM]
                o = u * (mixed[:, j * GM_HEAD_DIM:(j + 1) * GM_HEAD_DIM]
                         + bias_ref[:, lo:lo + GM_HEAD_DIM])
                gmo_ref[0, j * CHUNK:(j + 1) * CHUNK, lo:lo + GM_HEAD_DIM] = o.astype(BF16)


def _inproj_call(x, norm_w, sc, sh, w_xbc, w_dtT, conv_w, conv_b, gm_args, tm):
    b, l, d = x.shape
    nt = l // tm
    nc_tile = tm // CHUNK
    hb = tm // HALO
    n_hb = l // HALO
    per_batch = sc.shape[0] == b
    mod_map = (lambda i, t: (i, 0, 0)) if per_batch else (lambda i, t: (0, 0, 0))
    with_gm = gm_args is not None

    in_specs = [
        pl.BlockSpec((1, tm, d), lambda i, t: (i, t, 0)),
        pl.BlockSpec((1, HALO, d), lambda i, t: (i, jnp.maximum(t * hb - 1, 0), 0)),
        pl.BlockSpec((1, HALO, d), lambda i, t: (i, jnp.minimum((t + 1) * hb, n_hb - 1), 0)),
        _const_spec((1, d)),
        pl.BlockSpec((1, 1, d), mod_map),
        pl.BlockSpec((1, 1, d), mod_map),
        _const_spec(w_xbc.shape),
        _const_spec(w_dtT.shape),
        _const_spec(conv_w.shape),
        _const_spec(conv_b.shape),
    ]
    args = [x, x, x, norm_w, sc, sh, w_xbc, w_dtT, conv_w, conv_b]
    out_specs = [
        pl.BlockSpec((1, tm, CONV_CH), lambda i, t: (i, t, 0)),
        pl.BlockSpec((1, nc_tile, 2 * SSD_HEADS, CHUNK), lambda i, t: (i, t, 0, 0)),
    ]
    out_shape = [
        jax.ShapeDtypeStruct((b, l, CONV_CH), BF16),
        jax.ShapeDtypeStruct((b, l // CHUNK, 2 * SSD_HEADS, CHUNK), F32),
    ]
    if with_gm:
        w_gm, w_z, gm_norm_w, gm_ws, gm_bias = gm_args
        in_specs += [_const_spec(w_gm.shape), _const_spec(w_z.shape), _const_spec(gm_norm_w.shape),
                     _const_spec(gm_ws.shape), _const_spec(gm_bias.shape)]
        args += [w_gm, w_z, gm_norm_w, gm_ws, gm_bias]
        out_specs += [pl.BlockSpec((1, tm, GM_WIDTH), lambda i, t: (i, t, 0)),
                      pl.BlockSpec((1, tm, SSD_WIDTH), lambda i, t: (i, t, 0))]
        out_shape += [jax.ShapeDtypeStruct((b, l, GM_WIDTH), BF16),
                      jax.ShapeDtypeStruct((b, l, SSD_WIDTH), BF16)]
    return pl.pallas_call(
        functools.partial(_inproj_kernel, with_gm, tm, nt),
        grid=(b, nt),
        in_specs=in_specs,
        out_specs=out_specs,
        out_shape=out_shape,
        compiler_params=pltpu.CompilerParams(
            dimension_semantics=("arbitrary", "arbitrary"), vmem_limit_bytes=VMEM_LIMIT),
        name="inproj_lat" if with_gm else "inproj_ctx",
    )(*args)


def _dt_terms(dtT_raw, dtb_col, alog_col, tri_le, tri_ge):
    dt = _softplus(dtT_raw + dtb_col)
    a = dt * (-jnp.exp(alog_col))
    hi, mid, lo = _split3(a)
    cs = _dot(hi, tri_le) + _dot(mid, tri_le) + _dot(lo, tri_le)
    rs = _dot(hi, tri_ge) + _dot(mid, tri_ge) + _dot(lo, tri_ge)
    tot = jnp.sum(a, axis=-1, keepdims=True)
    return dt, cs, rs, tot


def _blockdiag(x_pair, lane_lo):
    zero = jnp.zeros_like(x_pair)
    return jnp.concatenate([jnp.where(lane_lo, x_pair, zero), jnp.where(lane_lo, zero, x_pair)], axis=0)


def _ssd_kernel(nc, ncc, xbc_ref, dtT_ref, xbcc_ref, dtTc_ref, dtb_ref, alog_ref, dexp_ref,
                y_ref, hf_all, hb_all, hf, hb):
    C = CHUNK
    H = SSD_HEADS
    r_i = lax.broadcasted_iota(jnp.int32, (C, C), 0)
    c_i = lax.broadcasted_iota(jnp.int32, (C, C), 1)
    tri_le = (r_i <= c_i).astype(BF16)
    tri_ge = (r_i >= c_i).astype(BF16)
    lane_lo = lax.broadcasted_iota(jnp.int32, (C, LANES), 1) < SSD_HEAD_DIM
    lane_lo_half = lax.broadcasted_iota(jnp.int32, (HALF, LANES), 1) < SSD_HEAD_DIM
    lane_lo_row = lax.broadcasted_iota(jnp.int32, (1, LANES), 1) < SSD_HEAD_DIM
    s_pos = [c_i % HALF + j * HALF for j in range(PAIR)]
    lower = [s_pos[j] <= r_i for j in range(PAIR)]
    diag = [s_pos[j] == r_i for j in range(PAIR)]
    e_i = lax.broadcasted_iota(jnp.int32, (2 * H, SSD_WIDTH), 0)
    l_i = lax.broadcasted_iota(jnp.int32, (2 * H, SSD_WIDTH), 1) // SSD_HEAD_DIM
    sel_f = e_i == l_i
    sel_b = e_i == l_i + H
    dtb_col = dtb_ref[...]
    alog_col = alog_ref[...]

    def state_update(x_ref, dtT_raw, r0, h_ref, fwd):
        dt, cs, rs, tot = _dt_terms(dtT_raw, dtb_col, alog_col, tri_le, tri_ge)
        w = dt * jnp.exp(tot - (cs if fwd else rs))
        dec = jnp.exp(jnp.sum(jnp.where(sel_f if fwd else sel_b, tot, 0.0), axis=0, keepdims=True))
        off = 0 if fwd else H
        for g in range(SSD_GROUPS):
            b_g = x_ref[0, pl.ds(r0, C), SSD_WIDTH + g * SSD_STATE:SSD_WIDTH + (g + 1) * SSD_STATE]
            bT = b_g.astype(F32).T
            for pp in range(PAIRS_PER_GROUP):
                p = g * PAIRS_PER_GROUP + pp
                e0 = p * PAIR
                lhs = jnp.concatenate(
                    [bT * w[off + e0 + q:off + e0 + q + 1, :] for q in range(PAIR)], axis=1).astype(BF16)
                x_pair = x_ref[0, pl.ds(r0, C), p * LANES:(p + 1) * LANES]
                s_pair = _dot(lhs, _blockdiag(x_pair, lane_lo))
                cols = slice(p * LANES, (p + 1) * LANES)
                h_ref[:, cols] = dec[:, cols] * h_ref[:, cols] + s_pair

    hf[...] = jnp.zeros_like(hf)
    hb[...] = jnp.zeros_like(hb)
    for c in range(ncc):
        state_update(xbcc_ref, dtTc_ref[0, c], c * C, hf, True)
    for c in range(ncc - 1, -1, -1):
        state_update(xbcc_ref, dtTc_ref[0, c], c * C, hb, False)

    def states_body(i, carry):
        cf = i
        hf_all[cf] = hf[...].astype(BF16)
        state_update(xbc_ref, dtT_ref[0, cf], pl.multiple_of(cf * C, C), hf, True)
        cb = nc - 1 - i
        hb_all[cb] = hb[...].astype(BF16)
        state_update(xbc_ref, dtT_ref[0, cb], pl.multiple_of(cb * C, C), hb, False)
        return carry

    lax.fori_loop(0, nc, states_body, 0)

    def pack_rows(v, vr, e1, e2, j):
        a, b = (v, vr) if j == 0 else (vr, v)
        return jnp.where(lane_lo_row, a[e1:e1 + 1, :], b[e2:e2 + 1, :])

    def out_body(c, carry):
        r0 = pl.multiple_of(c * C, C)
        dt, cs, rs, tot = _dt_terms(dtT_ref[0, c], dtb_col, alog_col, tri_le, tri_ge)
        cs2 = cs * LOG2E
        rs2 = rs * LOG2E
        lg = jnp.log(dt) * LOG2E
        srcs = [cs2 - lg, rs2 - lg, dt]
        rolled = [pltpu.roll(v, HALF, 1) for v in srcs]
        for g in range(SSD_GROUPS):
            gcols = slice(g * SSD_GROUP_WIDTH, (g + 1) * SSD_GROUP_WIDTH)
            b_g = xbc_ref[0, pl.ds(r0, C), SSD_WIDTH + g * SSD_STATE:SSD_WIDTH + (g + 1) * SSD_STATE]
            c_g = xbc_ref[0, pl.ds(r0, C),
                          SSD_WIDTH + BC_WIDTH + g * SSD_STATE:SSD_WIDTH + BC_WIDTH + (g + 1) * SSD_STATE]
            cb_dup = []
            for j in range(PAIR):
                b_half = b_g[j * HALF:(j + 1) * HALF]
                cb_dup.append(_dot_nt(c_g, jnp.concatenate([b_half, b_half], axis=0)))
            yoff_f = _dot(c_g, hf_all[c, :, gcols])
            yoff_b = _dot(c_g, hb_all[c, :, gcols])
            for pp in range(PAIRS_PER_GROUP):
                p = g * PAIRS_PER_GROUP + pp
                e1, e2 = p * PAIR, p * PAIR + 1
                col_f = jnp.concatenate([jnp.broadcast_to(cs2[e1:e1 + 1, :], (HALF, C)),
                                         jnp.broadcast_to(cs2[e2:e2 + 1, :], (HALF, C))], axis=0).T
                col_b = jnp.concatenate([jnp.broadcast_to(rs2[H + e1:H + e1 + 1, :], (HALF, C)),
                                         jnp.broadcast_to(rs2[H + e2:H + e2 + 1, :], (HALF, C))], axis=0).T
                m_blocks, x_blocks = [], []
                cols = slice(p * LANES, (p + 1) * LANES)
                x_pair = xbc_ref[0, pl.ds(r0, C), cols]
                for j in range(PAIR):
                    row_f = pack_rows(srcs[0], rolled[0], e1, e2, j)
                    row_b = pack_rows(srcs[1], rolled[1], H + e1, H + e2, j)
                    dsum = (pack_rows(srcs[2], rolled[2], e1, e2, j)
                            + pack_rows(srcs[2], rolled[2], H + e1, H + e2, j))
                    arg = jnp.where(lower[j], col_f - row_f, col_b - row_b)
                    decay = jnp.where(diag[j], dsum, jnp.exp2(arg))
                    m_blocks.append(cb_dup[j] * decay)
                    x_blocks.append(_blockdiag(x_pair[j * HALF:(j + 1) * HALF], lane_lo_half))
                lhs = jnp.concatenate(m_blocks, axis=1).astype(BF16)
                rhs = jnp.concatenate(x_blocks, axis=0)
                lcols = slice(pp * LANES, (pp + 1) * LANES)
                y = (_dot(lhs, rhs) + jnp.exp2(col_f) * yoff_f[:, lcols] + jnp.exp2(col_b) * yoff_b[:, lcols]
                     + x_pair.astype(F32) * dexp_ref[:, cols])
                y_ref[0, pl.ds(r0, C), cols] = y.astype(BF16)
        return carry

    lax.fori_loop(0, nc, out_body, 0)


def _ssd_call(xbc, dtT, xbc_c, dtT_c, dtb_col, alog_col, d_exp):
    b, l, _ = xbc.shape
    lc = xbc_c.shape[1]
    nc, ncc = l // CHUNK, lc // CHUNK
    return pl.pallas_call(
        functools.partial(_ssd_kernel, nc, ncc),
        grid=(b,),
        in_specs=[
            pl.BlockSpec((1, l, CONV_CH), lambda i: (i, 0, 0)),
            pl.BlockSpec((1, nc, 2 * SSD_HEADS, CHUNK), lambda i: (i, 0, 0, 0)),
            pl.BlockSpec((1, lc, CONV_CH), lambda i: (i, 0, 0)),
            pl.BlockSpec((1, ncc, 2 * SSD_HEADS, CHUNK), lambda i: (i, 0, 0, 0)),
            _const_spec(dtb_col.shape),
            _const_spec(alog_col.shape),
            _const_spec(d_exp.shape),
        ],
        out_specs=pl.BlockSpec((1, l, SSD_WIDTH), lambda i: (i, 0, 0)),
        out_shape=jax.ShapeDtypeStruct((b, l, SSD_WIDTH), BF16),
        scratch_shapes=[
            pltpu.VMEM((nc, SSD_STATE, SSD_WIDTH), BF16),
            pltpu.VMEM((nc, SSD_STATE, SSD_WIDTH), BF16),
            pltpu.VMEM((SSD_STATE, SSD_WIDTH), F32),
            pltpu.VMEM((SSD_STATE, SSD_WIDTH), F32),
        ],
        compiler_params=pltpu.CompilerParams(
            dimension_semantics=("arbitrary",), vmem_limit_bytes=VMEM_LIMIT),
        name="ssd_scan",
    )(xbc, dtT, xbc_c, dtT_c, dtb_col, alog_col, d_exp)


def _out_ffn_kernel(hid_chunks, x_ref, gm_ref, y_ref, z_ref, g1_ref, sc2_ref, sh2_ref, g2_ref, snw_ref,
                    n2w_ref, fnw_ref, woa_ref, wob_ref, w1_ref, w3_ref, w2_ref, o_ref):
    gz = y_ref[0].astype(F32) * _silu(z_ref[0].astype(F32))
    parts = []
    for g in range(SSD_GROUPS):
        cols = slice(g * SSD_GROUP_WIDTH, (g + 1) * SSD_GROUP_WIDTH)
        gg = gz[:, cols]
        ms = jnp.mean(gg * gg, axis=-1, keepdims=True)
        parts.append((gg * lax.rsqrt(ms + EPS) * snw_ref[:, cols]).astype(BF16))
    mix = _dot(gm_ref[0], woa_ref[...])
    for g in range(SSD_GROUPS):
        mix = mix + _dot(parts[g], wob_ref[g * SSD_GROUP_WIDTH:(g + 1) * SSD_GROUP_WIDTH, :])
    x1 = x_ref[0] + g1_ref[0] * mix
    h = _rms_mod(x1, n2w_ref[...] * (1.0 + sc2_ref[0]), sh2_ref[0]).astype(BF16)
    ffn = None
    lo = 0
    for width in hid_chunks:
        cols = slice(lo, lo + width)
        lo += width
        gate = _dot(h, w1_ref[:, cols])
        up = _dot(h, w3_ref[:, cols])
        part = _dot((_silu(gate) * up).astype(BF16), w2_ref[cols, :])
        ffn = part if ffn is None else ffn + part
    x2 = x1 + g2_ref[0] * ffn
    ms = jnp.mean(x2 * x2, axis=-1, keepdims=True)
    o_ref[0] = x2 * lax.rsqrt(ms + EPS) * fnw_ref[...]


def _hidden_chunks(hidden, n_chunks):
    passes = hidden // MXU_DEPTH
    assert passes * MXU_DEPTH == hidden
    base, extra = divmod(passes, n_chunks)
    return tuple((base + (1 if i < extra else 0)) * MXU_DEPTH for i in range(n_chunks))


def _out_ffn_call(x, gm_out, y, z, g1, sc2, sh2, g2, ssd_norm_w, norm2_w, final_w, wo_a, wo_b, w1, w3, w2, tm):
    b, l, d = x.shape
    tile = lambda width: pl.BlockSpec((1, tm, width), lambda i, t: (i, t, 0))
    mod = pl.BlockSpec((1, 1, d), lambda i, t: (i, 0, 0))
    return pl.pallas_call(
        functools.partial(_out_ffn_kernel, _hidden_chunks(w2.shape[0], 2)),
        grid=(b, l // tm),
        in_specs=[tile(d), tile(GM_WIDTH), tile(SSD_WIDTH), tile(SSD_WIDTH), mod, mod, mod, mod,
                  _const_spec(ssd_norm_w.shape), _const_spec(norm2_w.shape), _const_spec(final_w.shape),
                  _const_spec(wo_a.shape), _const_spec(wo_b.shape),
                  _const_spec(w1.shape), _const_spec(w3.shape), _const_spec(w2.shape)],
        out_specs=tile(d),
        out_shape=jax.ShapeDtypeStruct((b, l, d), F32),
        compiler_params=pltpu.CompilerParams(
            dimension_semantics=("arbitrary", "arbitrary"), vmem_limit_bytes=VMEM_LIMIT),
        name="outproj_ffn",
    )(x, gm_out, y, z, g1, sc2, sh2, g2, ssd_norm_w, norm2_w, final_w, wo_a, wo_b, w1, w3, w2)


def kernel(x, c, ctx, c_ctx, w_ada, b_ada, norm1_w, w_in, gm_norm_w, gm_ws, gm_bs, conv_w, conv_b,
           ssd_A_log, ssd_dt_bias, ssd_D, ssd_norm_w, w_out, norm2_w, ffn_w13, ffn_w2, final_norm_w):
    b, l, d = x.shape
    assert w_ada.shape[0] == 1, "single-layer block"
    assert d == D_MODEL and l % 512 == 0 and ctx.shape[1] % CHUNK == 0

    n_rows = -(-(b + 1) // 8) * 8
    c_pad = jnp.zeros((n_rows, d), F32).at[:b].set(c).at[b].set(c_ctx)
    mod = _ada_call(c_pad, w_ada, b_ada)
    sh1, sc1, g1, sh2, sc2, g2 = [m.reshape(b, 1, d) for m in jnp.split(mod[:b], N_MOD, axis=-1)]
    csh1, csc1 = [m.reshape(1, 1, d) for m in jnp.split(mod[b:b + 1], N_MOD, axis=-1)[:2]]

    w_in0 = w_in[0].astype(BF16)
    w_gm = w_in0[:, :2 * GM_WIDTH]
    w_z = w_in0[:, 2 * GM_WIDTH:2 * GM_WIDTH + SSD_WIDTH]
    w_xbc = w_in0[:, 2 * GM_WIDTH + SSD_WIDTH:2 * GM_WIDTH + SSD_WIDTH + CONV_CH]
    w_dtT = w_in0[:, 2 * GM_WIDTH + SSD_WIDTH + CONV_CH:].T
    n1w = norm1_w[0].reshape(1, d)
    cw = conv_w[0]
    cbias = conv_b[0].reshape(1, CONV_CH)
    gm_bias = jnp.repeat(gm_bs[0].T, GM_HEAD_DIM, axis=1)
    gm_args = (w_gm, w_z, gm_norm_w[0].reshape(1, GM_WIDTH), gm_ws[0].astype(BF16), gm_bias)
    dtb_col = ssd_dt_bias[0].reshape(2 * SSD_HEADS, 1)
    alog_col = ssd_A_log[0].reshape(2 * SSD_HEADS, 1)
    d_exp = jnp.repeat(ssd_D[0], SSD_HEAD_DIM).reshape(1, SSD_WIDTH)
    w_out0 = w_out[0].astype(BF16)
    w13 = ffn_w13[0].astype(BF16)

    xbc_c, dtT_c = _inproj_call(ctx, n1w, csc1, csh1, w_xbc, w_dtT, cw, cbias, None, ctx.shape[1])
    xbc, dtT, gm_out, z = _inproj_call(x, n1w, sc1, sh1, w_xbc, w_dtT, cw, cbias, gm_args, 512)
    y = _ssd_call(xbc, dtT, xbc_c, dtT_c, dtb_col, alog_col, d_exp)
    return _out_ffn_call(x, gm_out, y, z, g1, sc2, sh2, g2, ssd_norm_w[0].reshape(1, SSD_WIDTH),
                         norm2_w[0].reshape(1, d), final_norm_w.reshape(1, d),
                         w_out0[:GM_WIDTH], w_out0[GM_WIDTH:],
                         w13[:, :FFN_HIDDEN], w13[:, FFN_HIDDEN:], ffn_w2[0].astype(BF16), 512)
```
